```python
import math
import jax
import jax.numpy as jnp
from jax import lax
import numpy as np

D_MODEL = 2048
BATCH = 16
SEQ = 256
DEPTH = 4
DEC_BATCH = 2
DEC_SEQ = 2048
PAST_LEN = 256

GRID_W = 64
N_EVEN = (DEPTH + 1) // 2
N_ODD = DEPTH // 2
EPS = 1e-6
F32 = jnp.float32

A_WIDTH = D_MODEL // 2
A_DH = 64
A_HEADS = A_WIDTH // (2 * A_DH)
A_VD = 2 * A_DH
ROPE_THETA = 10000.0
Q_BLOCK = 128

B_WIDTH = D_MODEL // 2
B_HS = 64
B_HEADS = B_WIDTH // B_HS
B_DECAY_LORA = 64
B_AAA_LORA = 64
B_GATE_LORA = 128
B_GN_EPS = 64e-5
B_COLS = 3 * B_WIDTH + 2 * B_DECAY_LORA + 2 * B_AAA_LORA + B_GATE_LORA
E_IN = 3 * A_WIDTH + B_COLS

M_INNER = 2 * D_MODEL
M_HEADDIM = 64
M_HEADS = M_INNER // M_HEADDIM
M_STATE = 128
M_GROUPS = 8
M_CONV = 5
M_CHUNK = 128
M_CONV_CH = M_INNER + 2 * M_GROUPS * M_STATE
M_IN = M_INNER + M_CONV_CH + 2 * M_HEADS

N_EXPERTS = 16
CAPACITY = 2
E_FF = D_MODEL // 2

kernel_name = 'hybrid_diffattn_rwkv7_ssd_ecmoe_step'


def split_cols(u, sizes):
    return jnp.split(u, [int(s) for s in np.cumsum(sizes)[:-1]], axis=-1)


def rmsnorm(x, w=None, eps=EPS):
    xf = x.astype(F32)
    y = xf * lax.rsqrt(jnp.mean(xf * xf, axis=-1, keepdims=True) + eps)
    if w is not None:
        y = y * w.astype(F32)
    return y.astype(x.dtype)


def adaln(cond, w, b):
    m = jax.nn.silu(cond) @ w + b
    return [t[:, None, :] for t in jnp.split(m, 6, axis=-1)]


def centred_shift_mix(u, mu):
    prev = jnp.pad(u, ((0, 0), (1, 0), (0, 0)))[:, :-1]
    nxt = jnp.pad(u, ((0, 0), (0, 1), (0, 0)))[:, 1:]
    return u + (0.5 * (prev + nxt) - u) * mu


def axial_rope(rows):
    row = jnp.repeat(jnp.arange(rows, dtype=F32), GRID_W)
    col = jnp.tile(jnp.arange(GRID_W, dtype=F32), rows)
    n_freq = A_DH // 4
    inv = jnp.power(ROPE_THETA, -jnp.arange(n_freq, dtype=F32) / n_freq)
    ang = jnp.concatenate([row[:, None] * inv, col[:, None] * inv], axis=-1)
    return jnp.cos(ang), jnp.sin(ang)


def apply_rope(x, cos, sin):
    xp = x.astype(F32).reshape(x.shape[:-1] + (A_DH // 2, 2))
    c = cos[None, :, None, None, :]
    s = sin[None, :, None, None, :]
    x0, x1 = xp[..., 0], xp[..., 1]
    return jnp.stack([x0 * c - x1 * s, x0 * s + x1 * c], axis=-1).reshape(x.shape).astype(x.dtype)


def diff_attention(q, k, v, lam, lam_init, norm_w):
    bsz, tq = q.shape[:2]
    nb = tq // Q_BLOCK
    qb = jnp.moveaxis(q.reshape(bsz, nb, Q_BLOCK, A_HEADS, 2, A_DH), 1, 0)

    def block(qi):
        s = jnp.einsum('bqhmd,bkhmd->bhmqk', qi, k).astype(F32) * (A_DH ** -0.5)
        p = jax.nn.softmax(s, axis=-1)
        p = p[:, :, 0] - lam * p[:, :, 1]
        return jnp.einsum('bhqk,bkhd->bqhd', p.astype(v.dtype), v)

    o = jnp.moveaxis(lax.map(block, qb), 0, 1).reshape(bsz, tq, A_HEADS, A_VD)
    o = rmsnorm(o, norm_w) * (1.0 - lam_init)
    return o.reshape(bsz, tq, A_WIDTH)


def wkv_scan(r, w, k, v, kk, a, s0, reverse):
    def step(S, inp):
        r_t, w_t, k_t, v_t, kk_t, a_t = inp
        S = (S * w_t[:, :, None, :]
             - jnp.einsum('bhvk,bhk->bhv', S, kk_t)[..., None] * (kk_t * a_t)[:, :, None, :]
             + v_t[..., None] * k_t[:, :, None, :])
        return S, jnp.einsum('bhvk,bhk->bhv', S, r_t)
    xs = tuple(jnp.swapaxes(t, 0, 1) for t in (r, w, k, v, kk, a))
    s_fin, o = lax.scan(step, s0, xs, reverse=reverse)
    return jnp.swapaxes(o, 0, 1), s_fin


def rwkv7_mix(ub, s0, i, P):
    ub = centred_shift_mix(ub, P['b_mu'][i])
    r, k, v, wdf, wdb, adf, adb, gd = split_cols(ub, [B_WIDTH] * 3 + [B_DECAY_LORA] * 2 + [B_AAA_LORA] * 2 + [B_GATE_LORA])
    bsz, T = ub.shape[:2]

    def heads(t):
        return t.astype(F32).reshape(bsz, T, B_HEADS, B_HS)

    kk = heads(k * P['b_k_k'][i])
    kk = kk / jnp.maximum(jnp.sqrt(jnp.sum(kk * kk, axis=-1, keepdims=True)), 1e-12)
    rh, vh = heads(r), heads(v)
    o_sum, bonus, finals = 0.0, 0.0, []
    for d, (wd, ad) in enumerate(((wdf, adf), (wdb, adb))):
        w_log = -jax.nn.softplus(-(P['b_w0'][i, d] + jnp.tanh(wd) @ P['b_w2'][i, d])) - 0.5
        a = jax.nn.sigmoid(P['b_a0'][i, d] + ad @ P['b_a2'][i, d])
        kd = heads(k * (1.0 + (a - 1.0) * P['b_k_a'][i]))
        o, s_fin = wkv_scan(rh, jnp.exp(-jnp.exp(heads(w_log))), kd, vh, kk, heads(a), s0[:, d].astype(F32), d == 1)
        o_sum = o_sum + o
        bonus = bonus + jnp.sum(rh * kd * P['b_r_k'][i].astype(F32), axis=-1, keepdims=True) * vh
        finals.append(s_fin)
    mu = jnp.mean(o_sum, axis=-1, keepdims=True)
    var = jnp.mean(jnp.square(o_sum - mu), axis=-1, keepdims=True)
    gn = ((o_sum - mu) * lax.rsqrt(var + B_GN_EPS)).reshape(bsz, T, B_WIDTH) * P['b_ln_w'][i].astype(F32) + P['b_ln_b'][i].astype(F32)
    y = (gn + bonus.reshape(bsz, T, B_WIDTH)) * (jax.nn.sigmoid(gd) @ P['b_g2'][i]).astype(F32)
    return y.astype(ub.dtype), jnp.stack(finals, axis=1)


def even_mixer(h, i, layer, rope, ck, cv, s0, P):
    bsz, T = h.shape[:2]
    qa, ka, va, ub = split_cols(h @ P['e_w_in'][i], [A_WIDTH, A_WIDTH, A_WIDTH, B_COLS])
    q = qa.reshape(bsz, T, A_HEADS, 2, A_DH)
    k = ka.reshape(bsz, T, A_HEADS, 2, A_DH)
    v = va.reshape(bsz, T, A_HEADS, A_VD)
    if rope is None:
        keys, vals = k, v
    else:
        q = apply_rope(q, rope[0], rope[1])
        keys = jnp.concatenate([ck.reshape(bsz, -1, A_HEADS, 2, A_DH).astype(k.dtype), apply_rope(k, rope[0], rope[1])], axis=1)
        vals = jnp.concatenate([cv.astype(v.dtype), v], axis=1)
    lam_init = 0.8 - 0.6 * math.exp(-0.3 * layer)
    lq = P['a_lambda'][i].astype(F32)
    lam = jnp.exp(jnp.sum(lq[0] * lq[1])) - jnp.exp(jnp.sum(lq[2] * lq[3])) + lam_init
    ya = diff_attention(q, keys, vals, lam, lam_init, P['a_norm'][i])
    yb, s_new = rwkv7_mix(ub, s0, i, P)
    out = jnp.concatenate([ya, yb.astype(ya.dtype)], axis=-1) @ P['e_w_out'][i]
    return out, k.reshape(bsz, T, A_HEADS, 2 * A_DH), v, s_new.astype(h.dtype)


def dwconv_centred(u, w, b):
    pad = M_CONV // 2
    y = lax.conv_general_dilated(u, w[:, None, :].astype(u.dtype), window_strides=(1,), padding=[(pad, pad)],
                                 dimension_numbers=('NWC', 'WIO', 'NWC'), feature_group_count=u.shape[-1])
    return y + b.astype(u.dtype)


def segsum(a):
    T = a.shape[-1]
    ae = jnp.broadcast_to(a[..., None], a.shape + (T,))
    cs = jnp.cumsum(jnp.where(jnp.tril(jnp.ones((T, T), bool), -1), ae, 0.0), axis=-2)
    return jnp.where(jnp.tril(jnp.ones((T, T), bool)), cs, -jnp.inf)


def ssd_chunked(x, dt, A, bm, cm, s0):
    bsz, T = x.shape[:2]
    R = M_HEADS // M_GROUPS
    nc, L = T // M_CHUNK, M_CHUNK
    xdt = (x * dt[..., None]).reshape(bsz, nc, L, M_GROUPS, R, M_HEADDIM)
    a = jnp.transpose((dt * A).reshape(bsz, nc, L, M_GROUPS, R), (0, 3, 4, 1, 2))
    a_cum = jnp.cumsum(a, axis=-1)
    bc = bm.reshape(bsz, nc, L, M_GROUPS, M_STATE)
    cc = cm.reshape(bsz, nc, L, M_GROUPS, M_STATE)
    cb = jnp.einsum('bclgn,bcsgn->bcgls', cc, bc)
    y_diag = jnp.einsum('bcgls,bgrcls,bcsgrp->bclgrp', cb, jnp.exp(segsum(a)), xdt)
    decay_states = jnp.exp(a_cum[..., -1:] - a_cum)
    states = jnp.einsum('bclgn,bgrcl,bclgrp->bcgrpn', bc, decay_states, xdt)
    states = jnp.concatenate([s0.reshape(bsz, 1, M_GROUPS, R, M_HEADDIM, M_STATE), states], axis=1)
    chunk_a = jnp.pad(a_cum[..., -1], ((0, 0), (0, 0), (0, 0), (1, 0)))
    new_states = jnp.einsum('bgrzc,bcgrpn->bzgrpn', jnp.exp(segsum(chunk_a)), states)
    states, s_fin = new_states[:, :-1], new_states[:, -1]
    y_off = jnp.einsum('bclgn,bcgrpn,bgrcl->bclgrp', cc, states, jnp.exp(a_cum))
    return (y_diag + y_off).reshape(bsz, T, M_HEADS, M_HEADDIM), s_fin.reshape(bsz, M_HEADS, M_HEADDIM, M_STATE)


def odd_mixer(h, i, s0, P):
    bsz, T = h.shape[:2]
    z, xbc, dtf, dtb = split_cols(h @ P['m_w_in'][i], [M_INNER, M_CONV_CH, M_HEADS, M_HEADS])
    xbc = jax.nn.silu(dwconv_centred(xbc, P['m_conv_w'][i], P['m_conv_b'][i])).astype(F32)
    xs, bm, cm = split_cols(xbc, [M_INNER, M_GROUPS * M_STATE, M_GROUPS * M_STATE])
    xh = xs.reshape(bsz, T, M_HEADS, M_HEADDIM)
    bm = bm.reshape(bsz, T, M_GROUPS, M_STATE)
    cm = cm.reshape(bsz, T, M_GROUPS, M_STATE)
    y = xh * P['m_d'][i].astype(F32)[:, None]
    finals = []
    for d, dt_raw in enumerate((dtf, dtb)):
        dt = jax.nn.softplus(dt_raw.astype(F32) + P['m_dt_bias'][i, d].astype(F32))
        A = -jnp.exp(P['m_a_log'][i, d].astype(F32))
        if d == 0:
            yd, s_fin = ssd_chunked(xh, dt, A, bm, cm, s0[:, 0].astype(F32))
        else:
            yd, s_fin = ssd_chunked(xh[:, ::-1], dt[:, ::-1], A, bm[:, ::-1], cm[:, ::-1], s0[:, 1].astype(F32))
            yd = yd[:, ::-1]
        y = y + yd
        finals.append(s_fin)
    y = y.reshape(bsz, T, M_INNER) * jax.nn.silu(z.astype(F32))
    y = rmsnorm(y.reshape(bsz, T, M_GROUPS, M_INNER // M_GROUPS)).reshape(bsz, T, M_INNER) * P['m_norm'][i].astype(F32)
    return y.astype(h.dtype) @ P['m_w_out'][i], jnp.stack(finals, axis=1).astype(h.dtype)


def ec_moe(h, router, w_gate, w_up, w_down):
    bsz, T, D = h.shape
    cap = CAPACITY * T // N_EXPERTS
    aff = jax.nn.softmax((h @ router).astype(F32), axis=-1)
    g, idx = lax.top_k(jnp.swapaxes(aff, 1, 2), cap)
    xs = jax.vmap(lambda hb, ib: hb[ib])(h, idx)
    hid = jax.nn.silu(jnp.einsum('becd,edf->becf', xs, w_gate)) * jnp.einsum('becd,edf->becf', xs, w_up)
    ye = jnp.einsum('becf,efd->becd', hid, w_down) * g[..., None].astype(h.dtype)
    return jax.vmap(lambda ib, yb: jnp.zeros((T, D), yb.dtype).at[ib.reshape(-1)].add(yb.reshape(-1, D)))(idx, ye)


def trunk(x, cond, rope, ctx, P):
    bsz = x.shape[0]
    out_k, out_v, out_rw, out_ssm = [], [], [], []
    for layer in range(DEPTH):
        sh1, sc1, g1, sh2, sc2, g2 = adaln(cond, P['w_mod'][layer], P['b_mod'][layer])
        h = rmsnorm(x, P['norm_mix'][layer]) * (1 + sc1) + sh1
        i = layer // 2
        if layer % 2 == 0:
            if ctx is None:
                ck, cv = None, None
                s0 = jnp.zeros((bsz, 2, B_HEADS, B_HS, B_HS), F32)
            else:
                ck, cv, s0 = ctx[0][:, i], ctx[1][:, i], ctx[2][:, i]
            mix, k_new, v_new, s_new = even_mixer(h, i, layer, rope, ck, cv, s0, P)
            out_k.append(k_new)
            out_v.append(v_new)
            out_rw.append(s_new)
        else:
            s0 = jnp.zeros((bsz, 2, M_HEADS, M_HEADDIM, M_STATE), F32) if ctx is None else ctx[3][:, i]
            mix, s_new = odd_mixer(h, i, s0, P)
            out_ssm.append(s_new)
        x = x + g1 * mix
        h = rmsnorm(x, P['norm_ffn'][layer]) * (1 + sc2) + sh2
        x = x + g2 * ec_moe(h, P['moe_router'][layer], P['moe_w_gate'][layer], P['moe_w_up'][layer], P['moe_w_down'][layer])
    return rmsnorm(x, P['norm_final']), out_k, out_v, out_rw, out_ssm


def setup_inputs(seed: int = 0) -> dict:
    key = jax.random.key(seed)
    ks = iter(jax.random.split(key, 64))

    def nrm(shape, scale=1.0):
        return jax.random.normal(next(ks), shape, F32) * scale

    def unif(shape, lo, hi):
        return jax.random.uniform(next(ks), shape, F32, lo, hi)

    D = D_MODEL
    dt0 = jnp.exp(unif((N_ODD, 2, M_HEADS), math.log(1e-3), math.log(1e-1)))
    return {
        'x_prompt': nrm((BATCH, SEQ, D)),
        'x_sample': nrm((DEC_BATCH, DEC_SEQ, D)),
        'cache_attn_k': nrm((DEC_BATCH, N_EVEN, PAST_LEN, A_HEADS, 2 * A_DH)),
        'cache_attn_v': nrm((DEC_BATCH, N_EVEN, PAST_LEN, A_HEADS, A_VD)),
        'state_rwkv': nrm((DEC_BATCH, N_EVEN, 2, B_HEADS, B_HS, B_HS), 0.5),
        'state_ssm': nrm((DEC_BATCH, N_ODD, 2, M_HEADS, M_HEADDIM, M_STATE), 0.5),
        'c': nrm((DEC_BATCH, D)),
        'c_ctx': nrm((D,)),
        'w_mod': nrm((DEPTH, D, 6 * D), 0.5 * D ** -0.5),
        'b_mod': nrm((DEPTH, 6 * D), 0.01),
        'norm_mix': 1.0 + nrm((DEPTH, D), 0.02),
        'norm_ffn': 1.0 + nrm((DEPTH, D), 0.02),
        'norm_final': 1.0 + nrm((D,), 0.02),
        'e_w_in': nrm((N_EVEN, D, E_IN), D ** -0.5),
        'e_w_out': nrm((N_EVEN, A_WIDTH + B_WIDTH, D), (A_WIDTH + B_WIDTH) ** -0.5),
        'a_lambda': nrm((N_EVEN, 4, A_DH), 0.1),
        'a_norm': 1.0 + nrm((N_EVEN, A_VD), 0.02),
        'b_mu': unif((N_EVEN, B_COLS), 0.2, 0.8),
        'b_w0': nrm((N_EVEN, 2, B_WIDTH), 0.5),
        'b_w2': nrm((N_EVEN, 2, B_DECAY_LORA, B_WIDTH), B_DECAY_LORA ** -0.5),
        'b_a0': nrm((N_EVEN, 2, B_WIDTH), 0.5),
        'b_a2': nrm((N_EVEN, 2, B_AAA_LORA, B_WIDTH), B_AAA_LORA ** -0.5),
        'b_g2': nrm((N_EVEN, B_GATE_LORA, B_WIDTH), B_GATE_LORA ** -0.5),
        'b_k_k': 0.85 + nrm((N_EVEN, B_WIDTH), 0.02),
        'b_k_a': 1.0 + nrm((N_EVEN, B_WIDTH), 0.02),
        'b_r_k': nrm((N_EVEN, B_HEADS, B_HS), 0.1),
        'b_ln_w': 1.0 + nrm((N_EVEN, B_WIDTH), 0.02),
        'b_ln_b': nrm((N_EVEN, B_WIDTH), 0.01),
        'm_w_in': nrm((N_ODD, D, M_IN), D ** -0.5),
        'm_conv_w': nrm((N_ODD, M_CONV, M_CONV_CH), M_CONV ** -0.5),
        'm_conv_b': nrm((N_ODD, M_CONV_CH), 0.01),
        'm_dt_bias': dt0 + jnp.log(-jnp.expm1(-dt0)),
        'm_a_log': jnp.log(unif((N_ODD, 2, M_HEADS), 1.0, 16.0)),
        'm_d': 1.0 + nrm((N_ODD, M_HEADS), 0.1),
        'm_norm': 1.0 + nrm((N_ODD, M_INNER), 0.02),
        'm_w_out': nrm((N_ODD, M_INNER, D), M_INNER ** -0.5),
        'moe_router': nrm((DEPTH, D, N_EXPERTS), D ** -0.5),
        'moe_w_gate': nrm((DEPTH, N_EXPERTS, D, E_FF), D ** -0.5),
        'moe_w_up': nrm((DEPTH, N_EXPERTS, D, E_FF), D ** -0.5),
        'moe_w_down': nrm((DEPTH, N_EXPERTS, E_FF, D), E_FF ** -0.5),
    }


def reference(x_prompt, x_sample, cache_attn_k, cache_attn_v, state_rwkv, state_ssm, c, c_ctx,
              w_mod, b_mod, norm_mix, norm_ffn, norm_final, e_w_in, e_w_out, a_lambda, a_norm,
              b_mu, b_w0, b_w2, b_a0, b_a2, b_g2, b_k_k, b_k_a, b_r_k, b_ln_w, b_ln_b,
              m_w_in, m_conv_w, m_conv_b, m_dt_bias, m_a_log, m_d, m_norm, m_w_out,
              moe_router, moe_w_gate, moe_w_up, moe_w_down):
    P = {
        'w_mod': w_mod, 'b_mod': b_mod, 'norm_mix': norm_mix, 'norm_ffn': norm_ffn, 'norm_final': norm_final,
        'e_w_in': e_w_in, 'e_w_out': e_w_out, 'a_lambda': a_lambda, 'a_norm': a_norm,
        'b_mu': b_mu, 'b_w0': b_w0, 'b_w2': b_w2, 'b_a0': b_a0, 'b_a2': b_a2, 'b_g2': b_g2,
        'b_k_k': b_k_k, 'b_k_a': b_k_a, 'b_r_k': b_r_k, 'b_ln_w': b_ln_w, 'b_ln_b': b_ln_b,
        'm_w_in': m_w_in, 'm_conv_w': m_conv_w, 'm_conv_b': m_conv_b, 'm_dt_bias': m_dt_bias,
        'm_a_log': m_a_log, 'm_d': m_d, 'm_norm': m_norm, 'm_w_out': m_w_out,
        'moe_router': moe_router, 'moe_w_gate': moe_w_gate, 'moe_w_up': moe_w_up, 'moe_w_down': moe_w_down,
    }
    y_prompt, pk, pv, prw, pssm = trunk(x_prompt, c_ctx[None, :], None, None, P)
    new_attn_k = jnp.stack(pk, axis=1)
    new_attn_v = jnp.stack(pv, axis=1)
    new_state_rwkv = jnp.stack(prw, axis=1)
    new_state_ssm = jnp.stack(pssm, axis=1)
    rows = x_sample.shape[1] // GRID_W
    rope = axial_rope(rows)
    y_sample, _, _, _, _ = trunk(x_sample, c, rope, (cache_attn_k, cache_attn_v, state_rwkv, state_ssm), P)
    return (y_prompt, y_sample, new_attn_k, new_attn_v, new_state_rwkv, new_state_ssm)
```

```python
import functools
import math
from typing import NamedTuple

import jax
import jax.numpy as jnp
import numpy as np
from jax import lax
from jax.experimental import pallas as pl
from jax.experimental.pallas import tpu as pltpu

F32 = jnp.float32
BF16 = jnp.bfloat16
HIGHEST = lax.Precision.HIGHEST

D_MODEL = 2048
DEPTH = 4
GRID_W = 64
EPS = 1e-6

A_WIDTH = D_MODEL // 2
A_DH = 64
A_HEADS = A_WIDTH // (2 * A_DH)
A_VD = 2 * A_DH
ROPE_THETA = 10000.0

B_WIDTH = D_MODEL // 2
B_HS = 64
B_HEADS = B_WIDTH // B_HS
B_GN_EPS = 64e-5
B_COLS = 3 * B_WIDTH + 2 * 64 + 2 * 64 + 128
E_IN = 3 * A_WIDTH + B_COLS
RW_CHUNK = 64
RW_SUB = 16

M_INNER = 2 * D_MODEL
M_HEADDIM = 64
M_HEADS = M_INNER // M_HEADDIM
M_STATE = 128
M_GROUPS = 8
M_CONV = 5
M_CHUNK = 128
M_CONV_CH = M_INNER + 2 * M_GROUPS * M_STATE
M_GW = M_INNER // M_GROUPS
M_GH = M_HEADS // M_GROUPS

N_EXPERTS = 16
CAPACITY = 2
E_FF = D_MODEL // 2

LANES = 128
MOD_ROWS = 16
ROW_TILE = 1024
VMEM_LIMIT = 52 * 1024 * 1024


class Cfg(NamedTuple):
    n_ctx: int
    l_ctx: int
    n_dec: int
    l_dec: int
    past: int

    @property
    def ctx_rows(self):
        return self.n_ctx * self.l_ctx

    @property
    def ntok(self):
        return self.n_ctx * self.l_ctx + self.n_dec * self.l_dec

    def groups(self):
        return ((0, self.n_ctx, self.l_ctx, False), (self.ctx_rows, self.n_dec, self.l_dec, True))


def _params(sem):
    return pltpu.CompilerParams(dimension_semantics=sem, vmem_limit_bytes=VMEM_LIMIT)


def _row_tile(cfg):
    return ROW_TILE if cfg.ctx_rows % ROW_TILE == 0 and cfg.l_dec % ROW_TILE == 0 else 256


def _modrow(cfg, start):
    return jnp.where(start < cfg.ctx_rows, 0, 1 + (start - cfg.ctx_rows) // cfg.l_dec)


def _bdot(a, b):
    return jnp.dot(a.astype(BF16), b.astype(BF16), preferred_element_type=F32)


def _bdot_nt(a, b):
    return lax.dot_general(a.astype(BF16), b.astype(BF16), (((1,), (1,)), ((), ())), preferred_element_type=F32)


def _softplus(z):
    return jnp.maximum(z, 0.0) + jnp.log(1.0 + jnp.exp(-jnp.abs(z)))


def _sigmoid(z):
    return 1.0 / (1.0 + jnp.exp(-z))


def _silu(z):
    return z * _sigmoid(z)


def _mod_kernel(c_ref, w_ref, b_ref, o_ref):
    o_ref[...] = _bdot(_silu(c_ref[...]), w_ref[...]) + b_ref[...]


def compute_mods(cond, w_mod, b_mod):
    depth, d, n = w_mod.shape
    nr = cond.shape[0]
    tn = 1024
    out = pl.pallas_call(
        _mod_kernel,
        grid=(depth, n // tn),
        in_specs=[
            pl.BlockSpec((nr, d), lambda l, j: (0, 0)),
            pl.BlockSpec((None, d, tn), lambda l, j: (l, 0, j)),
            pl.BlockSpec((None, 1, tn), lambda l, j: (l, 0, j)),
        ],
        out_specs=pl.BlockSpec((None, nr, tn), lambda l, j: (l, 0, j)),
        out_shape=jax.ShapeDtypeStruct((depth, nr, n), F32),
        compiler_params=_params(("parallel", "parallel")),
        name="adaln_mods",
    )(cond, w_mod, b_mod.reshape(depth, 1, n))
    return out.reshape(depth, nr, 1, n)


def _mod_spec(cfg, layer, chunk, tm, d, ngrid):
    if ngrid == 1:
        return pl.BlockSpec((None, None, 1, d), lambda i: (layer, _modrow(cfg, i * tm), 0, chunk))
    return pl.BlockSpec((None, None, 1, d), lambda i, j: (layer, _modrow(cfg, i * tm), 0, chunk))


def _norm_mod(x, w, sc, sh):
    y = x * lax.rsqrt(jnp.mean(x * x, axis=-1, keepdims=True) + EPS) * w
    return y * (1.0 + sc) + sh


def _norm_mod_kernel(x_ref, w_ref, sc_ref, sh_ref, o_ref):
    o_ref[...] = _norm_mod(x_ref[...], w_ref[...], sc_ref[...], sh_ref[...]).astype(o_ref.dtype)


def _norm_router_kernel(x_ref, w_ref, sc_ref, sh_ref, r_ref, o_ref, aff_ref):
    h = _norm_mod(x_ref[...], w_ref[...], sc_ref[...], sh_ref[...])
    o_ref[...] = h.astype(o_ref.dtype)
    lg = jnp.dot(h, r_ref[...], precision=HIGHEST, preferred_element_type=F32)
    lane = lax.broadcasted_iota(jnp.int32, lg.shape, 1)
    lg = jnp.where(lane < N_EXPERTS, lg, -1e30)
    e = jnp.exp(lg - jnp.max(lg, axis=-1, keepdims=True))
    aff_ref[...] = e / jnp.sum(e, axis=-1, keepdims=True)


def norm_mod(cfg, x, w, mods, layer, sc_chunk, sh_chunk, router=None):
    m, d = x.shape
    tm = 256
    specs = [
        pl.BlockSpec((tm, d), lambda i: (i, 0)),
        pl.BlockSpec((1, d), lambda i: (0, 0)),
        _mod_spec(cfg, layer, sc_chunk, tm, d, 1),
        _mod_spec(cfg, layer, sh_chunk, tm, d, 1),
    ]
    if router is None:
        return pl.pallas_call(
            _norm_mod_kernel, grid=(m // tm,), in_specs=specs,
            out_specs=pl.BlockSpec((tm, d), lambda i: (i, 0)),
            out_shape=jax.ShapeDtypeStruct((m, d), BF16),
            compiler_params=_params(("parallel",)), name="norm_mod",
        )(x, w.reshape(1, d), mods, mods)
    rpad = jnp.zeros((d, LANES), F32).at[:, :N_EXPERTS].set(router)
    return pl.pallas_call(
        _norm_router_kernel, grid=(m // tm,),
        in_specs=specs + [pl.BlockSpec((d, LANES), lambda i: (0, 0))],
        out_specs=[pl.BlockSpec((tm, d), lambda i: (i, 0)), pl.BlockSpec((tm, LANES), lambda i: (i, 0))],
        out_shape=[jax.ShapeDtypeStruct((m, d), BF16), jax.ShapeDtypeStruct((m, LANES), F32)],
        compiler_params=_params(("parallel",)), name="norm_mod_router",
    )(x, w.reshape(1, d), mods, mods, rpad)


def _final_norm_kernel(x_ref, w_ref, o_ref):
    x = x_ref[...]
    o_ref[...] = x * lax.rsqrt(jnp.mean(x * x, axis=-1, keepdims=True) + EPS) * w_ref[...]


def final_norm(x, w):
    m, d = x.shape
    tm = 256
    return pl.pallas_call(
        _final_norm_kernel, grid=(m // tm,),
        in_specs=[pl.BlockSpec((tm, d), lambda i: (i, 0)), pl.BlockSpec((1, d), lambda i: (0, 0))],
        out_specs=pl.BlockSpec((tm, d), lambda i: (i, 0)),
        out_shape=jax.ShapeDtypeStruct((m, d), F32),
        compiler_params=_params(("parallel",)), name="final_norm",
    )(x, w.reshape(1, d))


def _mm_kernel(a_ref, b_ref, o_ref):
    o_ref[...] = _bdot(a_ref[...], b_ref[...]).astype(o_ref.dtype)


def matmul(a, b, tm, tn):
    m, k = a.shape
    n = b.shape[1]
    return pl.pallas_call(
        _mm_kernel, grid=(m // tm, n // tn),
        in_specs=[pl.BlockSpec((tm, k), lambda i, j: (i, 0)), pl.BlockSpec((k, tn), lambda i, j: (0, j))],
        out_specs=pl.BlockSpec((tm, tn), lambda i, j: (i, j)),
        out_shape=jax.ShapeDtypeStruct((m, n), F32),
        compiler_params=_params(("parallel", "parallel")), name="proj_in",
    )(a, b)


def _mm_res_kernel(a_ref, b_ref, x_ref, g_ref, o_ref):
    o_ref[...] = x_ref[...] + g_ref[...] * _bdot(a_ref[...], b_ref[...])


def matmul_residual(cfg, a, b, x, mods, layer, gate_chunk, tm, tn):
    m, k = a.shape
    n = b.shape[1]
    gspec = pl.BlockSpec((None, None, 1, tn), lambda i, j: (layer, _modrow(cfg, i * tm), 0, gate_chunk * (n // tn) + j))
    return pl.pallas_call(
        _mm_res_kernel, grid=(m // tm, n // tn),
        in_specs=[pl.BlockSpec((tm, k), lambda i, j: (i, 0)), pl.BlockSpec((k, tn), lambda i, j: (0, j)),
                  pl.BlockSpec((tm, tn), lambda i, j: (i, j)), gspec],
        out_specs=pl.BlockSpec((tm, tn), lambda i, j: (i, j)),
        out_shape=jax.ShapeDtypeStruct((m, n), F32),
        compiler_params=_params(("parallel", "parallel")), name="proj_out_residual",
    )(a, b, x, mods)


def _rope(x, cos, sin_signed):
    lane = lax.broadcasted_iota(jnp.int32, x.shape, 1)
    swapped = jnp.where(lane % 2 == 0, pltpu.roll(x, LANES - 1, 1), pltpu.roll(x, 1, 1))
    return x * cos + swapped * sin_signed


def _softmax_rows(s):
    e = jnp.exp(s - jnp.max(s, axis=-1, keepdims=True))
    return e / jnp.sum(e, axis=-1, keepdims=True)


def _diff_attn(q, kb, vb, lam, nw, lam_init):
    lane = lax.broadcasted_iota(jnp.int32, q.shape, 1)
    q0 = jnp.where(lane < A_DH, q, 0.0)
    q1 = jnp.where(lane >= A_DH, q, 0.0)
    p = _softmax_rows(_bdot_nt(q0, kb)) - lam * _softmax_rows(_bdot_nt(q1, kb))
    o = jnp.dot(p.astype(BF16), vb, preferred_element_type=F32)
    o = o * lax.rsqrt(jnp.mean(o * o, axis=-1, keepdims=True) + EPS) * nw
    return o * (1.0 - lam_init)


def _attn_ctx_kernel(lam_ref, q_ref, k_ref, v_ref, nw_ref, o_ref, *, lam_init):
    q = q_ref[...] * (A_DH ** -0.5)
    o = _diff_attn(q, k_ref[...].astype(BF16), v_ref[...].astype(BF16), lam_ref[0], nw_ref[...], lam_init)
    o_ref[...] = o.astype(o_ref.dtype)


def attention_ctx(cfg, proj, lam, norm_w, lam_init):
    t = cfg.l_ctx
    nh = A_HEADS
    return pl.pallas_call(
        functools.partial(_attn_ctx_kernel, lam_init=lam_init),
        grid=(cfg.n_ctx, nh),
        in_specs=[
            pl.BlockSpec(memory_space=pltpu.SMEM),
            pl.BlockSpec((t, LANES), lambda s, h: (s, h)),
            pl.BlockSpec((t, LANES), lambda s, h: (s, nh + h)),
            pl.BlockSpec((t, LANES), lambda s, h: (s, 2 * nh + h)),
            pl.BlockSpec((1, LANES), lambda s, h: (0, 0)),
        ],
        out_specs=pl.BlockSpec((t, LANES), lambda s, h: (s, h)),
        out_shape=jax.ShapeDtypeStruct((cfg.ctx_rows, A_WIDTH), BF16),
        compiler_params=_params(("parallel", "parallel")), name="diff_attn_ctx",
    )(lam, proj, proj, proj, norm_w.reshape(1, LANES))


def _attn_dec_kernel(lam_ref, q_ref, k_ref, v_ref, ck_ref, cv_ref, cos_ref, sin_ref, cosq_ref, sinq_ref, nw_ref,
                     o_ref, ks, vs, *, past, lam_init):
    @pl.when(pl.program_id(2) == 0)
    def _():
        ks[0:past, :] = ck_ref[...].astype(BF16)
        vs[0:past, :] = cv_ref[...].astype(BF16)
        ks[past:, :] = _rope(k_ref[...], cos_ref[...], sin_ref[...]).astype(BF16)
        vs[past:, :] = v_ref[...].astype(BF16)

    q = _rope(q_ref[...], cosq_ref[...], sinq_ref[...]) * (A_DH ** -0.5)
    o = _diff_attn(q, ks[...], vs[...], lam_ref[0], nw_ref[...], lam_init)
    o_ref[...] = o.astype(o_ref.dtype)


def attention_dec(cfg, proj, cache_k, cache_v, cos, sin_signed, lam, norm_w, lam_init):
    t = cfg.l_dec
    nh = A_HEADS
    tq = 256 if t % 256 == 0 else 128
    blk0 = cfg.ctx_rows // t
    qblk0 = cfg.ctx_rows // tq
    nq = t // tq
    return pl.pallas_call(
        functools.partial(_attn_dec_kernel, past=cfg.past, lam_init=lam_init),
        grid=(cfg.n_dec, nh, nq),
        in_specs=[
            pl.BlockSpec(memory_space=pltpu.SMEM),
            pl.BlockSpec((tq, LANES), lambda b, h, qi: (qblk0 + b * nq + qi, h)),
            pl.BlockSpec((t, LANES), lambda b, h, qi: (blk0 + b, nh + h)),
            pl.BlockSpec((t, LANES), lambda b, h, qi: (blk0 + b, 2 * nh + h)),
            pl.BlockSpec((None, cfg.past, LANES), lambda b, h, qi: (b, 0, h)),
            pl.BlockSpec((None, cfg.past, LANES), lambda b, h, qi: (b, 0, h)),
            pl.BlockSpec((t, LANES), lambda b, h, qi: (0, 0)),
            pl.BlockSpec((t, LANES), lambda b, h, qi: (0, 0)),
            pl.BlockSpec((tq, LANES), lambda b, h, qi: (qi, 0)),
            pl.BlockSpec((tq, LANES), lambda b, h, qi: (qi, 0)),
            pl.BlockSpec((1, LANES), lambda b, h, qi: (0, 0)),
        ],
        out_specs=pl.BlockSpec((tq, LANES), lambda b, h, qi: (b * nq + qi, h)),
        out_shape=jax.ShapeDtypeStruct((cfg.n_dec * t, A_WIDTH), BF16),
        scratch_shapes=[pltpu.VMEM((cfg.past + t, LANES), BF16), pltpu.VMEM((cfg.past + t, LANES), BF16)],
        compiler_params=_params(("parallel", "parallel", "arbitrary")), name="diff_attn_dec",
    )(lam, proj, proj, proj, cache_k, cache_v, cos, sin_signed, cos, sin_signed, norm_w.reshape(1, LANES))


def rope_tables(t):
    rows = t // GRID_W
    row = jnp.repeat(jnp.arange(rows, dtype=F32), GRID_W)
    col = jnp.tile(jnp.arange(GRID_W, dtype=F32), rows)
    n_freq = A_DH // 4
    inv = jnp.power(ROPE_THETA, -jnp.arange(n_freq, dtype=F32) / n_freq)
    ang = jnp.concatenate([row[:, None] * inv, col[:, None] * inv], axis=-1)
    cos = jnp.repeat(jnp.cos(ang), 2, axis=-1)
    sin = jnp.repeat(jnp.sin(ang), 2, axis=-1) * jnp.tile(jnp.array([-1.0, 1.0], F32), A_DH // 2)
    return jnp.tile(cos, (1, 2)), jnp.tile(sin, (1, 2))


def _shift_kernel(u_ref, mu_ref, o_ref):
    u = u_ref[...]
    t = u.shape[0]
    row = lax.broadcasted_iota(jnp.int32, u.shape, 0)
    prev = jnp.where(row == 0, 0.0, pltpu.roll(u, 1, 0))
    nxt = jnp.where(row == t - 1, 0.0, pltpu.roll(u, t - 1, 0))
    o_ref[...] = u + (0.5 * (prev + nxt) - u) * mu_ref[...]


def rw_shift(cfg, proj, mu):
    tc = 384
    col0 = (3 * A_WIDTH) // tc
    ncol = B_COLS // tc
    outs = []
    for row0, nseq, t, _ in cfg.groups():
        blk0 = row0 // t
        outs.append(pl.pallas_call(
            _shift_kernel, grid=(nseq, ncol),
            in_specs=[pl.BlockSpec((t, tc), lambda s, j: (blk0 + s, col0 + j)),
                      pl.BlockSpec((1, tc), lambda s, j: (0, j))],
            out_specs=pl.BlockSpec((t, tc), lambda s, j: (s, j)),
            out_shape=jax.ShapeDtypeStruct((nseq * t, B_COLS), F32),
            compiler_params=_params(("parallel", "parallel")), name="rwkv_token_shift",
        )(proj, mu.reshape(1, B_COLS)))
    return jnp.concatenate(outs, axis=0)


def _rw_lora_kernel(u_ref, w0_ref, w2_ref, a0_ref, a2_ref, g2_ref, ci_ref, ce_ref, a_ref, gate_ref):
    c0 = 3 * B_WIDTH
    wd = jnp.tanh(u_ref[:, c0:c0 + 128])
    ad = u_ref[:, c0 + 128:c0 + 256]
    gd = u_ref[:, c0 + 256:c0 + 384]
    xw = w0_ref[...] + _bdot(wd, w2_ref[...])
    lw = -jnp.exp(-_softplus(-xw) - 0.5)
    a = _sigmoid(a0_ref[...] + _bdot(ad, a2_ref[...]))
    gate_ref[...] = _bdot(_sigmoid(gd), g2_ref[...])
    tm = lw.shape[0]
    row = lax.broadcasted_iota(jnp.int32, (tm, tm), 0)
    col = lax.broadcasted_iota(jnp.int32, (tm, tm), 1)
    same = (row // RW_CHUNK) == (col // RW_CHUNK)
    for d in range(2):
        tri = jnp.where(same & ((col <= row) if d == 0 else (col >= row)), 1.0, 0.0)
        lwd = lw[:, d * B_WIDTH:(d + 1) * B_WIDTH]
        ci = jnp.dot(tri, lwd, precision=HIGHEST, preferred_element_type=F32)
        ci_ref[d] = ci
        ce_ref[d] = ci - lwd
        a_ref[d] = a[:, d * B_WIDTH:(d + 1) * B_WIDTH]


def _blockdiag2(w):
    r, c = w.shape[1:]
    z = jnp.zeros((r, c), w.dtype)
    return jnp.concatenate([jnp.concatenate([w[0], z], axis=1), jnp.concatenate([z, w[1]], axis=1)], axis=0)


def rw_lora(u, w0, w2, a0, a2, g2):
    m = u.shape[0]
    tm = 256
    bw = B_WIDTH
    full = lambda r, c: pl.BlockSpec((r, c), lambda i: (0, 0))
    dspec = pl.BlockSpec((2, tm, bw), lambda i: (0, i, 0))
    dshape = jax.ShapeDtypeStruct((2, m, bw), F32)
    return pl.pallas_call(
        _rw_lora_kernel, grid=(m // tm,),
        in_specs=[pl.BlockSpec((tm, B_COLS), lambda i: (i, 0)), full(1, 2 * bw), full(128, 2 * bw),
                  full(1, 2 * bw), full(128, 2 * bw), full(128, bw)],
        out_specs=[dspec, dspec, dspec, pl.BlockSpec((tm, bw), lambda i: (i, 0))],
        out_shape=[dshape, dshape, dshape, jax.ShapeDtypeStruct((m, bw), F32)],
        compiler_params=_params(("parallel",)), name="rwkv_lora",
    )(u, w0.reshape(1, 2 * bw), _blockdiag2(w2), a0.reshape(1, 2 * bw), _blockdiag2(a2), g2)


def _tri_inverse(a, blk, eye):
    ident = jnp.where(eye, 1.0, 0.0)
    x = jnp.where(blk, a, 0.0)
    off = a - x
    inv_bd = ident - x
    p = x
    for _ in range(int(math.log2(RW_SUB)) - 1):
        p = _bdot(p, p)
        inv_bd = _bdot(inv_bd, ident + p)
    n = _bdot(inv_bd, off)
    out = ident - n
    p = n
    for _ in range(int(math.log2(RW_CHUNK // RW_SUB)) - 1):
        p = _bdot(p, p)
        out = _bdot(out, ident + p)
    return _bdot(out, inv_bd)


def _rw_scan_kernel(r_ref, k_ref, v_ref, a_ref, ci_ref, ce_ref, s0_ref, kk_ref, ka_ref, rk_ref,
                    o_ref, bon_ref, sf_ref, state):
    c = RW_CHUNK
    fwd = pl.program_id(1) == 0

    @pl.when(pl.program_id(2) == 0)
    def _():
        state[...] = s0_ref[...]

    row = lax.broadcasted_iota(jnp.int32, (c, c), 0)
    col = lax.broadcasted_iota(jnp.int32, (c, c), 1)
    lag = (row - col) * jnp.where(fwd, 1, -1)
    strict = lag > 0
    incl = lag >= 0
    eye = row == col
    blk = (row // RW_SUB) == (col // RW_SUB)
    ident = jnp.where(eye, 1.0, 0.0)

    def head(h, carry):
        r, k, v, a, ci, ce = r_ref[h], k_ref[h], v_ref[h], a_ref[h], ci_ref[h], ce_ref[h]
        kkp = k * kk_ref[h]
        kap = kkp / jnp.maximum(jnp.sqrt(jnp.sum(kkp * kkp, axis=-1, keepdims=True)), 1e-12)
        kd = k * (1.0 + (a - 1.0) * ka_ref[h])
        beta = kap * a
        bon_ref[h] = jnp.sum(r * kd * rk_ref[h], axis=-1, keepdims=True) * v
        tot = jnp.where(fwd, ci[c - 1:c, :], ci[0:1, :])
        ein = jnp.exp(-ci)
        kt = kap * jnp.exp(ce)
        bt = beta * ein
        kdt = kd * ein
        rt = r * jnp.exp(ci)
        erest = jnp.exp(tot - ci)
        amat = jnp.where(strict, _bdot_nt(kt, bt), 0.0)
        mmat = jnp.where(strict, _bdot_nt(kt, kdt), 0.0)
        rb = jnp.where(incl, _bdot_nt(rt, bt), 0.0)
        rk = jnp.where(incl, _bdot_nt(rt, kdt), 0.0)
        inv = _tri_inverse(amat, blk, eye)
        s_old = state[h]
        p = _bdot(inv, kt)
        q = _bdot(inv, _bdot(mmat, v))
        u = -_bdot_nt(p, s_old) - q
        o_ref[h] = _bdot_nt(rt, s_old) + _bdot(rb, u) + _bdot(rk, v)
        ut = _bdot_nt(ident, u)
        vt = _bdot_nt(ident, v)
        state[h] = s_old * jnp.exp(tot) + _bdot(ut, beta * erest) + _bdot(vt, kd * erest)
        return carry

    lax.fori_loop(0, B_HEADS, head, 0)
    sf_ref[...] = state[...]


def rw_scan(nseq, t, r, k, v, a, ci, ce, s0, k_k, k_a, r_k):
    nh, hs, c = B_HEADS, B_HS, RW_CHUNK
    nc = t // c
    cidx = lambda d, ch: jnp.where(d == 0, ch, nc - 1 - ch)
    tok = pl.BlockSpec((None, nh, c, hs), lambda s, d, ch: (s, 0, cidx(d, ch), 0))
    dtok = pl.BlockSpec((None, None, nh, c, hs), lambda s, d, ch: (d, s, 0, cidx(d, ch), 0))
    st = pl.BlockSpec((None, None, nh, hs, hs), lambda s, d, ch: (s, d, 0, 0, 0))
    par = pl.BlockSpec((nh, 1, hs), lambda s, d, ch: (0, 0, 0))
    o, bon, sf = pl.pallas_call(
        _rw_scan_kernel, grid=(nseq, 2, nc),
        in_specs=[tok, tok, tok, dtok, dtok, dtok, st, par, par, par],
        out_specs=[dtok, dtok, st],
        out_shape=[jax.ShapeDtypeStruct((2, nseq, nh, t, hs), F32), jax.ShapeDtypeStruct((2, nseq, nh, t, hs), F32),
                   jax.ShapeDtypeStruct((nseq, 2, nh, hs, hs), F32)],
        scratch_shapes=[pltpu.VMEM((nh, hs, hs), F32)],
        compiler_params=_params(("parallel", "parallel", "arbitrary")), name="rwkv_scan",
    )(r, k, v, a, ci, ce, s0, k_k.reshape(nh, 1, hs), k_a.reshape(nh, 1, hs), r_k.reshape(nh, 1, hs))
    return o, bon, sf


def _rw_post_kernel(o_ref, bon_ref, lnw_ref, lnb_ref, y_ref):
    o = o_ref[0] + o_ref[1]
    mu = jnp.mean(o, axis=-1, keepdims=True)
    var = jnp.mean(jnp.square(o - mu), axis=-1, keepdims=True)
    gn = (o - mu) * lax.rsqrt(var + B_GN_EPS) * lnw_ref[...] + lnb_ref[...]
    y_ref[...] = gn + bon_ref[0] + bon_ref[1]


def rw_post(nseq, t, o, bon, ln_w, ln_b):
    nh, hs = B_HEADS, B_HS
    tt = min(t, 256)
    dspec = pl.BlockSpec((2, None, nh, tt, hs), lambda s, j: (0, s, 0, j, 0))
    par = pl.BlockSpec((nh, 1, hs), lambda s, j: (0, 0, 0))
    return pl.pallas_call(
        _rw_post_kernel, grid=(nseq, t // tt),
        in_specs=[dspec, dspec, par, par],
        out_specs=pl.BlockSpec((None, nh, tt, hs), lambda s, j: (s, 0, j, 0)),
        out_shape=jax.ShapeDtypeStruct((nseq, nh, t, hs), F32),
        compiler_params=_params(("parallel", "parallel")), name="rwkv_groupnorm",
    )(o, bon, ln_w.reshape(nh, 1, hs), ln_b.reshape(nh, 1, hs))


def _assemble_kernel(ya_ref, yb_ref, gate_ref, o_ref):
    o_ref[:, :A_WIDTH] = ya_ref[...]
    o_ref[:, A_WIDTH:] = (yb_ref[...] * gate_ref[...]).astype(o_ref.dtype)


def assemble(ya, yb, gate):
    m = ya.shape[0]
    tm = 512
    half = pl.BlockSpec((tm, A_WIDTH), lambda i: (i, 0))
    return pl.pallas_call(
        _assemble_kernel, grid=(m // tm,), in_specs=[half, half, half],
        out_specs=pl.BlockSpec((tm, A_WIDTH + B_WIDTH), lambda i: (i, 0)),
        out_shape=jax.ShapeDtypeStruct((m, A_WIDTH + B_WIDTH), BF16),
        compiler_params=_params(("parallel",)), name="even_assemble",
    )(ya, yb, gate)


def _head_major(x, nseq, t):
    return x.reshape(nseq, t, B_HEADS, B_HS).transpose(0, 2, 1, 3)


def _head_major2(x, nseq, t):
    return x.reshape(2, nseq, t, B_HEADS, B_HS).transpose(0, 1, 3, 2, 4)


def rwkv_mixer(cfg, proj, i, states0, P):
    u = rw_shift(cfg, proj, P['b_mu'][i])
    ci, ce, a, gate = rw_lora(u, P['b_w0'][i], P['b_w2'][i], P['b_a0'][i], P['b_a2'][i], P['b_g2'][i])
    ys, finals = [], []
    for (row0, nseq, t, is_dec), s0 in zip(cfg.groups(), states0):
        rows = slice(row0, row0 + nseq * t)
        r = _head_major(u[rows, 0:B_WIDTH], nseq, t)
        k = _head_major(u[rows, B_WIDTH:2 * B_WIDTH], nseq, t)
        v = _head_major(u[rows, 2 * B_WIDTH:3 * B_WIDTH], nseq, t)
        o, bon, sf = rw_scan(nseq, t, r, k, v, _head_major2(a[:, rows], nseq, t), _head_major2(ci[:, rows], nseq, t),
                             _head_major2(ce[:, rows], nseq, t), s0, P['b_k_k'][i], P['b_k_a'][i], P['b_r_k'][i])
        y = rw_post(nseq, t, o, bon, P['b_ln_w'][i], P['b_ln_b'][i])
        ys.append(y.transpose(0, 2, 1, 3).reshape(nseq * t, B_WIDTH))
        finals.append(sf)
    return jnp.concatenate(ys, axis=0), gate, finals[0]


def _conv_kernel(u_ref, w_ref, b_ref, o_ref):
    u = u_ref[...]
    t = u.shape[0]
    row = lax.broadcasted_iota(jnp.int32, u.shape, 0)
    acc = u * w_ref[2:3, :] + b_ref[...]
    for off in (1, 2):
        before = jnp.where(row < off, 0.0, pltpu.roll(u, off, 0))
        after = jnp.where(row >= t - off, 0.0, pltpu.roll(u, t - off, 0))
        acc = acc + before * w_ref[2 - off:3 - off, :] + after * w_ref[2 + off:3 + off, :]
    o_ref[...] = _silu(acc)


def ssd_conv(cfg, proj, conv_w, conv_b, col_lo, width):
    tc = 512
    pcol0 = (M_INNER + col_lo) // tc
    wcol0 = col_lo // tc
    outs = []
    for row0, nseq, t, _ in cfg.groups():
        blk0 = row0 // t
        outs.append(pl.pallas_call(
            _conv_kernel, grid=(nseq, width // tc),
            in_specs=[pl.BlockSpec((t, tc), lambda s, j: (blk0 + s, pcol0 + j)),
                      pl.BlockSpec((M_CONV, tc), lambda s, j: (0, wcol0 + j)),
                      pl.BlockSpec((1, tc), lambda s, j: (0, wcol0 + j))],
            out_specs=pl.BlockSpec((t, tc), lambda s, j: (s, j)),
            out_shape=jax.ShapeDtypeStruct((nseq * t, width), F32),
            compiler_params=_params(("parallel", "parallel")), name="ssd_conv",
        )(proj, conv_w, conv_b.reshape(1, M_CONV_CH)))
    return jnp.concatenate(outs, axis=0)


def _ssd_scan_kernel(xs_ref, xst_ref, b_ref, c_ref, dt_ref, dtt_ref, bias_ref, biast_ref, al_ref, alt_ref, s0_ref,
                     y_ref, sf_ref, state, *, nchunk):
    ln = M_CHUNK
    hd = M_HEADDIM
    fwd = pl.program_id(1) == 0

    @pl.when(pl.program_id(3) == 0)
    def _():
        state[...] = s0_ref[...]

    dt = _softplus(dt_ref[...] + bias_ref[...])
    a = dt * -jnp.exp(al_ref[...])
    dtt = _softplus(dtt_ref[...] + biast_ref[...])
    at = dtt * -jnp.exp(alt_ref[...])
    row = lax.broadcasted_iota(jnp.int32, (ln, ln), 0)
    col = lax.broadcasted_iota(jnp.int32, (ln, ln), 1)
    sign = jnp.where(fwd, 1, -1)
    incl = (row - col) * sign >= 0
    incl_t = (col - row) * sign >= 0
    cum = jnp.dot(jnp.where(incl, 1.0, 0.0), a, precision=HIGHEST, preferred_element_type=F32)
    cumt = jnp.dot(at, jnp.where(incl_t, 1.0, 0.0), precision=HIGHEST, preferred_element_type=F32)
    tot = jnp.where(fwd, cum[ln - 1:ln, :], cum[0:1, :])
    bmat = b_ref[...]
    cmat = c_ref[...]
    cb = _bdot_nt(cmat, bmat)
    lane = lax.broadcasted_iota(jnp.int32, (ln, 2 * hd), 1)
    for pr in range(M_GH // 2):
        xpair = xs_ref[:, pr * 2 * hd:(pr + 1) * 2 * hd]
        s_old = state[pr * 2 * hd:(pr + 1) * 2 * hd, :]
        ypair = jnp.zeros((ln, 2 * hd), F32)
        for q in range(2):
            h = 2 * pr + q
            ccol = cum[:, h:h + 1]
            lmat = jnp.where(incl, jnp.exp(jnp.minimum(ccol - cumt[h:h + 1, :], 0.0)), 0.0)
            mine = (lane >= hd) if q == 1 else (lane < hd)
            xdt = jnp.where(mine, xpair * dt[:, h:h + 1], 0.0)
            ypair = ypair + _bdot(cb * lmat, xdt)
            bdec = bmat * jnp.exp(tot[:, h:h + 1] - ccol)
            xdtt = xst_ref[h * hd:(h + 1) * hd, :] * dtt[h:h + 1, :]
            state[h * hd:(h + 1) * hd, :] = s_old[q * hd:(q + 1) * hd, :] * jnp.exp(tot[:, h:h + 1]) + _bdot(xdtt, bdec)
        escale = jnp.where(lane < hd, jnp.exp(cum[:, 2 * pr:2 * pr + 1]), jnp.exp(cum[:, 2 * pr + 1:2 * pr + 2]))
        y_ref[:, pr * 2 * hd:(pr + 1) * 2 * hd] = ypair + _bdot_nt(cmat, s_old) * escale

    @pl.when(pl.program_id(3) == nchunk - 1)
    def _():
        sf_ref[...] = state[...]


def ssd_scan(row0, nseq, t, xs, xst, bm, cm, dt, dtt, bias, biast, alog, alogt, s0):
    ln, gw, gh, ns = M_CHUNK, M_GW, M_GH, M_STATE
    nc = t // ln
    blk0 = row0 // ln

    def rb(s, d, c):
        return blk0 + s * nc + jnp.where(d == 0, c, nc - 1 - c)

    def rb_out(s, d, c):
        return s * nc + jnp.where(d == 0, c, nc - 1 - c)

    st = pl.BlockSpec((None, None, gw, ns), lambda s, d, g, c: (s, d, g, 0))
    return pl.pallas_call(
        functools.partial(_ssd_scan_kernel, nchunk=nc), grid=(nseq, 2, M_GROUPS, nc),
        in_specs=[
            pl.BlockSpec((ln, gw), lambda s, d, g, c: (rb(s, d, c), g)),
            pl.BlockSpec((gw, ln), lambda s, d, g, c: (g, rb(s, d, c))),
            pl.BlockSpec((ln, ns), lambda s, d, g, c: (rb(s, d, c), g)),
            pl.BlockSpec((ln, ns), lambda s, d, g, c: (rb(s, d, c), g)),
            pl.BlockSpec((None, None, ln, gh), lambda s, d, g, c: (d, g, rb(s, d, c), 0)),
            pl.BlockSpec((None, None, gh, ln), lambda s, d, g, c: (d, g, 0, rb(s, d, c))),
            pl.BlockSpec((None, None, 1, gh), lambda s, d, g, c: (d, g, 0, 0)),
            pl.BlockSpec((None, None, gh, 1), lambda s, d, g, c: (d, g, 0, 0)),
            pl.BlockSpec((None, None, 1, gh), lambda s, d, g, c: (d, g, 0, 0)),
            pl.BlockSpec((None, None, gh, 1), lambda s, d, g, c: (d, g, 0, 0)),
            st,
        ],
        out_specs=[pl.BlockSpec((None, ln, gw), lambda s, d, g, c: (d, rb_out(s, d, c), g)), st],
        out_shape=[jax.ShapeDtypeStruct((2, nseq * t, M_INNER), F32),
                   jax.ShapeDtypeStruct((nseq, 2, M_INNER, ns), F32)],
        scratch_shapes=[pltpu.VMEM((gw, ns), F32)],
        compiler_params=_params(("parallel", "parallel", "parallel", "arbitrary")), name="ssd_scan",
    )(xs, xst, bm, cm, dt, dtt, bias, biast, alog, alogt, s0)


def _ssd_post_kernel(xs_ref, y_ref, z_ref, dvec_ref, nw_ref, o_ref):
    y = xs_ref[...] * dvec_ref[...] + y_ref[0] + y_ref[1]
    y = y * _silu(z_ref[...])
    y = y * lax.rsqrt(jnp.mean(y * y, axis=-1, keepdims=True) + EPS) * nw_ref[...]
    o_ref[...] = y.astype(o_ref.dtype)


def ssd_post(xs, y, proj, dvec, norm_w):
    m = xs.shape[0]
    tm = 256
    gw = M_GW
    blk = pl.BlockSpec((tm, gw), lambda i, g: (i, g))
    vec = pl.BlockSpec((1, gw), lambda i, g: (0, g))
    return pl.pallas_call(
        _ssd_post_kernel, grid=(m // tm, M_GROUPS),
        in_specs=[blk, pl.BlockSpec((2, tm, gw), lambda i, g: (0, i, g)), blk, vec, vec],
        out_specs=blk, out_shape=jax.ShapeDtypeStruct((m, M_INNER), BF16),
        compiler_params=_params(("parallel", "parallel")), name="ssd_gated_norm",
    )(xs, y, proj, dvec.reshape(1, M_INNER), norm_w.reshape(1, M_INNER))


def ssd_mixer(cfg, proj, i, states0, P):
    xs = ssd_conv(cfg, proj, P['m_conv_w'][i], P['m_conv_b'][i], 0, M_INNER)
    bm = ssd_conv(cfg, proj, P['m_conv_w'][i], P['m_conv_b'][i], M_INNER, M_GROUPS * M_STATE)
    cm = ssd_conv(cfg, proj, P['m_conv_w'][i], P['m_conv_b'][i], M_INNER + M_GROUPS * M_STATE, M_GROUPS * M_STATE)
    xst = xs.T
    ntok = xs.shape[0]
    c0 = M_INNER + M_CONV_CH
    dtr = proj[:, c0:c0 + 2 * M_HEADS].reshape(ntok, 2, M_GROUPS, M_GH)
    dt = dtr.transpose(1, 2, 0, 3)
    dtt = dtr.transpose(1, 2, 3, 0)
    bias = P['m_dt_bias'][i].reshape(2, M_GROUPS, 1, M_GH)
    alog = P['m_a_log'][i].reshape(2, M_GROUPS, 1, M_GH)
    ys, finals = [], []
    for (row0, nseq, t, _), s0 in zip(cfg.groups(), states0):
        y, sf = ssd_scan(row0, nseq, t, xs, xst, bm, cm, dt, dtt, bias, bias.transpose(0, 1, 3, 2),
                         alog, alog.transpose(0, 1, 3, 2), s0)
        ys.append(y)
        finals.append(sf)
    y = jnp.concatenate(ys, axis=1)
    dvec = jnp.repeat(P['m_d'][i], M_HEADDIM)
    return ssd_post(xs, y, proj, dvec, P['m_norm'][i]), finals[0]


def _rank_kernel(aff_ref, afft_ref, rank_ref, *, ts):
    j = pl.program_id(1)

    @pl.when(j == 0)
    def _():
        rank_ref[...] = jnp.zeros_like(rank_ref)

    t = afft_ref.shape[1]
    other = lax.broadcasted_iota(jnp.int32, (ts, t), 0) + j * ts
    me = lax.broadcasted_iota(jnp.int32, (ts, t), 1)
    earlier = other < me
    for e in range(N_EXPERTS):
        colv = aff_ref[:, e:e + 1]
        rowv = afft_ref[e:e + 1, :]
        ahead = (colv > rowv) | ((colv == rowv) & earlier)
        rank_ref[e:e + 1, :] += jnp.sum(jnp.where(ahead, 1.0, 0.0), axis=0, keepdims=True)


def moe_rank(row0, nseq, t, aff, afft):
    ts = min(t, 256)
    nj = t // ts
    blk0 = row0 // ts
    tblk0 = row0 // t
    return pl.pallas_call(
        functools.partial(_rank_kernel, ts=ts), grid=(nseq, nj),
        in_specs=[pl.BlockSpec((ts, LANES), lambda s, j: (blk0 + s * nj + j, 0)),
                  pl.BlockSpec((N_EXPERTS, t), lambda s, j: (0, tblk0 + s))],
        out_specs=pl.BlockSpec((None, N_EXPERTS, t), lambda s, j: (s, 0, 0)),
        out_shape=jax.ShapeDtypeStruct((nseq, N_EXPERTS, t), F32),
        compiler_params=_params(("parallel", "arbitrary")), name="moe_rank",
    )(aff, afft)


def _dispatch_kernel(rank_ref, h_ref, aff_ref, xs_ref, g_ref, *, cap):
    e = pl.program_id(1)
    t = h_ref.shape[0]
    slot = lax.broadcasted_iota(jnp.int32, (cap, t), 0).astype(F32)
    onehot = jnp.where(rank_ref[...] == slot, 1.0, 0.0)
    xs_ref[...] = jnp.dot(onehot.astype(BF16), h_ref[...], preferred_element_type=F32).astype(xs_ref.dtype)
    gall = jnp.dot(onehot, aff_ref[...], precision=HIGHEST, preferred_element_type=F32)
    lane = lax.broadcasted_iota(jnp.int32, gall.shape, 1)
    g = jnp.sum(jnp.where(lane == e, gall, 0.0), axis=-1, keepdims=True)
    g_ref[...] = jnp.broadcast_to(g, g_ref.shape)


def moe_dispatch(row0, nseq, t, rank4, h, aff):
    cap = CAPACITY * t // N_EXPERTS
    d = h.shape[1]
    blk0 = row0 // t
    return pl.pallas_call(
        functools.partial(_dispatch_kernel, cap=cap), grid=(nseq, N_EXPERTS),
        in_specs=[pl.BlockSpec((None, None, 1, t), lambda s, e: (s, e, 0, 0)),
                  pl.BlockSpec((t, d), lambda s, e: (blk0 + s, 0)),
                  pl.BlockSpec((t, LANES), lambda s, e: (blk0 + s, 0))],
        out_specs=[pl.BlockSpec((None, cap, d), lambda s, e: (e, s, 0)),
                   pl.BlockSpec((None, cap, LANES), lambda s, e: (e, s, 0))],
        out_shape=[jax.ShapeDtypeStruct((N_EXPERTS, nseq * cap, d), BF16),
                   jax.ShapeDtypeStruct((N_EXPERTS, nseq * cap, LANES), F32)],
        compiler_params=_params(("parallel", "arbitrary")), name="moe_dispatch",
    )(rank4, h, aff)


def _expert_kernel(xa_ref, xb_ref, ga_ref, gb_ref, wg_ref, wu_ref, wd_ref, ya_ref, yb_ref, *, nf):
    f = pl.program_id(1)
    wg = wg_ref[...].astype(BF16)
    wu = wu_ref[...].astype(BF16)
    wd = wd_ref[...].astype(BF16)
    for x_ref, g_ref, y_ref in ((xa_ref, ga_ref, ya_ref), (xb_ref, gb_ref, yb_ref)):
        x = x_ref[...]
        hid = _silu(jnp.dot(x, wg, preferred_element_type=F32)) * jnp.dot(x, wu, preferred_element_type=F32)
        part = jnp.dot(hid.astype(BF16), wd, preferred_element_type=F32)

        @pl.when(f == 0)
        def _():
            y_ref[...] = part

        @pl.when(f > 0)
        def _():
            y_ref[...] += part

        @pl.when(f == nf - 1)
        def _():
            y_ref[...] = y_ref[...] * g_ref[:, 0:1]


def moe_experts(xa, xb, ga, gb, w_gate, w_up, w_down):
    ne, ra, d = xa.shape
    rb = xb.shape[1]
    ff = w_gate.shape[2]
    tf = 256
    nf = ff // tf
    row = lambda r, w: pl.BlockSpec((None, r, w), lambda e, f: (e, 0, 0))
    return pl.pallas_call(
        functools.partial(_expert_kernel, nf=nf), grid=(ne, nf),
        in_specs=[row(ra, d), row(rb, d), row(ra, LANES), row(rb, LANES),
                  pl.BlockSpec((None, d, tf), lambda e, f: (e, 0, f)),
                  pl.BlockSpec((None, d, tf), lambda e, f: (e, 0, f)),
                  pl.BlockSpec((None, tf, d), lambda e, f: (e, f, 0))],
        out_specs=[row(ra, d), row(rb, d)],
        out_shape=[jax.ShapeDtypeStruct((ne, ra, d), F32), jax.ShapeDtypeStruct((ne, rb, d), F32)],
        compiler_params=_params(("parallel", "arbitrary")), name="moe_experts",
    )(xa, xb, ga, gb, w_gate, w_up, w_down)


def _combine_kernel(rankt_ref, ye_ref, x_ref, g_ref, o_ref, *, cap):
    e = pl.program_id(2)
    t = x_ref.shape[0]

    @pl.when(e == 0)
    def _():
        o_ref[...] = jnp.zeros_like(o_ref)

    lane = lax.broadcasted_iota(jnp.int32, rankt_ref.shape, 1)
    rank = jnp.sum(jnp.where(lane == e, rankt_ref[...], 0.0), axis=-1, keepdims=True)
    slot = lax.broadcasted_iota(jnp.int32, (t, cap), 1).astype(F32)
    onehot_t = jnp.where(rank == slot, 1.0, 0.0)
    o_ref[...] += _bdot(onehot_t, ye_ref[...])

    @pl.when(e == N_EXPERTS - 1)
    def _():
        o_ref[...] = x_ref[...] + g_ref[...] * o_ref[...]


def moe_combine(cfg, row0, nseq, t, rankt, ye, x, mods, layer):
    cap = CAPACITY * t // N_EXPERTS
    d = x.shape[1]
    td = 512
    blk0 = row0 // t
    gspec = pl.BlockSpec((None, None, 1, td), lambda s, j, e: (layer, _modrow(cfg, row0 + s * t), 0, 5 * (d // td) + j))
    return pl.pallas_call(
        functools.partial(_combine_kernel, cap=cap), grid=(nseq, d // td, N_EXPERTS),
        in_specs=[pl.BlockSpec((t, LANES), lambda s, j, e: (s, 0)),
                  pl.BlockSpec((None, cap, td), lambda s, j, e: (e, s, j)),
                  pl.BlockSpec((t, td), lambda s, j, e: (blk0 + s, j)), gspec],
        out_specs=pl.BlockSpec((t, td), lambda s, j, e: (s, j)),
        out_shape=jax.ShapeDtypeStruct((nseq * t, d), F32),
        compiler_params=_params(("parallel", "parallel", "arbitrary")), name="moe_combine",
    )(rankt, ye, x, mods)


def moe_layer(cfg, x, mods, layer, P):
    h, aff = norm_mod(cfg, x, P['norm_ffn'][layer], mods, layer, 4, 3, router=P['moe_router'][layer])
    afft = aff[:, :N_EXPERTS].T
    xs, gs, rts = [], [], []
    for row0, nseq, t, _ in cfg.groups():
        rank = moe_rank(row0, nseq, t, aff, afft)
        xe, ge = moe_dispatch(row0, nseq, t, rank.reshape(nseq, N_EXPERTS, 1, t), h, aff)
        rt = jnp.zeros((nseq * t, LANES), F32).at[:, :N_EXPERTS].set(rank.transpose(0, 2, 1).reshape(nseq * t, N_EXPERTS))
        xs.append(xe)
        gs.append(ge)
        rts.append(rt)
    yes = moe_experts(xs[0], xs[1], gs[0], gs[1], P['moe_w_gate'][layer], P['moe_w_up'][layer], P['moe_w_down'][layer])
    outs = [moe_combine(cfg, row0, nseq, t, rt, ye, x, mods, layer)
            for (row0, nseq, t, _), rt, ye in zip(cfg.groups(), rts, yes)]
    return jnp.concatenate(outs, axis=0)


def _lambda(a_lambda_i, layer):
    lam_init = 0.8 - 0.6 * math.exp(-0.3 * layer)
    lq = a_lambda_i.astype(F32)
    lam = jnp.exp(jnp.sum(lq[0] * lq[1])) - jnp.exp(jnp.sum(lq[2] * lq[3])) + lam_init
    return lam.reshape(1), lam_init


def even_layer(cfg, x, mods, layer, cache_k, cache_v, state_rwkv, rope, P):
    i = layer // 2
    h = norm_mod(cfg, x, P['norm_mix'][layer], mods, layer, 1, 0)
    proj = matmul(h, P['e_w_in'][i], _row_tile(cfg), 384)
    lam, lam_init = _lambda(P['a_lambda'][i], layer)
    ya_ctx = attention_ctx(cfg, proj, lam, P['a_norm'][i], lam_init)
    ck = cache_k[:, i].reshape(cfg.n_dec, cfg.past, A_WIDTH)
    cv = cache_v[:, i].reshape(cfg.n_dec, cfg.past, A_WIDTH)
    ya_dec = attention_dec(cfg, proj, ck, cv, rope[0], rope[1], lam, P['a_norm'][i], lam_init)
    s0_ctx = jnp.zeros((cfg.n_ctx, 2, B_HEADS, B_HS, B_HS), F32)
    yb, gate, s_new = rwkv_mixer(cfg, proj, i, (s0_ctx, state_rwkv[:, i]), P)
    mix_in = assemble(jnp.concatenate([ya_ctx, ya_dec], axis=0), yb, gate)
    x = matmul_residual(cfg, mix_in, P['e_w_out'][i], x, mods, layer, 2, _row_tile(cfg), 512)
    nc = cfg.ctx_rows
    k_new = proj[:nc, A_WIDTH:2 * A_WIDTH].reshape(cfg.n_ctx, cfg.l_ctx, A_HEADS, 2 * A_DH)
    v_new = proj[:nc, 2 * A_WIDTH:3 * A_WIDTH].reshape(cfg.n_ctx, cfg.l_ctx, A_HEADS, A_VD)
    return x, k_new, v_new, s_new


def odd_layer(cfg, x, mods, layer, state_ssm, P):
    i = layer // 2
    h = norm_mod(cfg, x, P['norm_mix'][layer], mods, layer, 1, 0)
    proj = matmul(h, P['m_w_in'][i], _row_tile(cfg), 384)
    s0_ctx = jnp.zeros((cfg.n_ctx, 2, M_INNER, M_STATE), F32)
    s0_dec = state_ssm[:, i].reshape(cfg.n_dec, 2, M_INNER, M_STATE)
    y, s_new = ssd_mixer(cfg, proj, i, (s0_ctx, s0_dec), P)
    x = matmul_residual(cfg, y, P['m_w_out'][i], x, mods, layer, 2, _row_tile(cfg), 256)
    return x, s_new.reshape(cfg.n_ctx, 2, M_HEADS, M_HEADDIM, M_STATE)


def trunk(cfg, x_prompt, x_sample, cache_attn_k, cache_attn_v, state_rwkv, state_ssm, c, c_ctx, P):
    d = x_prompt.shape[-1]
    x = jnp.concatenate([x_prompt.reshape(-1, d), x_sample.reshape(-1, d)], axis=0)
    cond = jnp.zeros((MOD_ROWS, d), F32).at[0].set(c_ctx).at[1:1 + cfg.n_dec].set(c)
    mods = compute_mods(cond, P['w_mod'], P['b_mod'])
    rope = rope_tables(cfg.l_dec)
    ks, vs, rws, ssms = [], [], [], []
    for layer in range(DEPTH):
        if layer % 2 == 0:
            x, k_new, v_new, s_new = even_layer(cfg, x, mods, layer, cache_attn_k, cache_attn_v, state_rwkv, rope, P)
            ks.append(k_new)
            vs.append(v_new)
            rws.append(s_new)
        else:
            x, s_new = odd_layer(cfg, x, mods, layer, state_ssm, P)
            ssms.append(s_new)
        x = moe_layer(cfg, x, mods, layer, P)
    y = final_norm(x, P['norm_final'])
    y_prompt = y[:cfg.ctx_rows].reshape(x_prompt.shape)
    y_sample = y[cfg.ctx_rows:].reshape(x_sample.shape)
    return (y_prompt, y_sample, jnp.stack(ks, axis=1), jnp.stack(vs, axis=1), jnp.stack(rws, axis=1),
            jnp.stack(ssms, axis=1))


def kernel(x_prompt, x_sample, cache_attn_k, cache_attn_v, state_rwkv, state_ssm, c, c_ctx, w_mod, b_mod, norm_mix, norm_ffn, norm_final, e_w_in, e_w_out, a_lambda, a_norm, b_mu, b_w0, b_w2, b_a0, b_a2, b_g2, b_k_k, b_k_a, b_r_k, b_ln_w, b_ln_b, m_w_in, m_conv_w, m_conv_b, m_dt_bias, m_a_log, m_d, m_norm, m_w_out, moe_router, moe_w_gate, moe_w_up, moe_w_down):
    P = {
        'w_mod': w_mod, 'b_mod': b_mod, 'norm_mix': norm_mix, 'norm_ffn': norm_ffn, 'norm_final': norm_final,
        'e_w_in': e_w_in, 'e_w_out': e_w_out, 'a_lambda': a_lambda, 'a_norm': a_norm,
        'b_mu': b_mu, 'b_w0': b_w0, 'b_w2': b_w2, 'b_a0': b_a0, 'b_a2': b_a2, 'b_g2': b_g2,
        'b_k_k': b_k_k, 'b_k_a': b_k_a, 'b_r_k': b_r_k, 'b_ln_w': b_ln_w, 'b_ln_b': b_ln_b,
        'm_w_in': m_w_in, 'm_conv_w': m_conv_w, 'm_conv_b': m_conv_b, 'm_dt_bias': m_dt_bias,
        'm_a_log': m_a_log, 'm_d': m_d, 'm_norm': m_norm, 'm_w_out': m_w_out,
        'moe_router': moe_router, 'moe_w_gate': moe_w_gate, 'moe_w_up': moe_w_up, 'moe_w_down': moe_w_down,
    }
    cfg = Cfg(n_ctx=x_prompt.shape[0], l_ctx=x_prompt.shape[1], n_dec=x_sample.shape[0], l_dec=x_sample.shape[1],
              past=cache_attn_k.shape[2])
    return trunk(cfg, x_prompt, x_sample, cache_attn_k, cache_attn_v, state_rwkv, state_ssm, c, c_ctx, P)
```
